```python
import math
import jax, jax.numpy as jnp
from jax import lax
import numpy as np

D_MODEL = 1024
BATCH = 2
SEQ = 8192
DEPTH = 1

MEM_LEN = 256
SSD_EXPAND = 2
SSD_D_INNER = SSD_EXPAND * D_MODEL
SSD_HEAD_DIM = 64
SSD_HEADS = SSD_D_INNER // SSD_HEAD_DIM
SSD_GROUPS = 4
SSD_HEADS_PER_GROUP = SSD_HEADS // SSD_GROUPS
SSD_STATE = 128
SSD_CONV = 4
SSD_CHUNK = 128
SSD_CONV_DIM = SSD_D_INNER + 2 * SSD_GROUPS * SSD_STATE
GMLP_WIDTH = D_MODEL
GMLP_CHUNK = 128
GMLP_GROUPS = 8
GMLP_GROUP_DIM = GMLP_WIDTH // GMLP_GROUPS
MEM_HEADS = 4
MEM_HEAD_DIM = 64
MEM_WIDTH = MEM_HEADS * MEM_HEAD_DIM
D_FF = ((8 * D_MODEL // 3 + 127) // 128) * 128
N_BRANCH = 3
EPS = 1e-6

_IN_SIZES = (SSD_D_INNER, SSD_CONV_DIM, SSD_HEADS, 2 * GMLP_WIDTH, MEM_WIDTH, N_BRANCH * D_MODEL)
IN_WIDTH = sum(_IN_SIZES)
_IN_SPLITS = tuple(sum(_IN_SIZES[:i + 1]) for i in range(len(_IN_SIZES) - 1))

kernel_name = "hybrid_ssd_gmlp_memory_gated_block"


def rmsnorm(x, g):
    xf = x.astype(jnp.float32)
    y = xf * lax.rsqrt(jnp.mean(xf * xf, axis=-1, keepdims=True) + EPS)
    return (y * g.astype(jnp.float32)).astype(x.dtype)


def swiglu(x, w_gate, w_up, w_down):
    return (jax.nn.silu(x @ w_gate) * (x @ w_up)) @ w_down


def causal_dwconv(x, w, b):
    k, c = w.shape
    y = lax.conv_general_dilated(
        x, w[:, None, :].astype(x.dtype), window_strides=(1,), padding=[(k - 1, 0)],
        dimension_numbers=("NWC", "WIO", "NWC"), feature_group_count=c)
    return y + b.astype(x.dtype)


def ssd_chunked(xh, dt, a, bm, cm):
    bsz, s, h, p = xh.shape
    g, n = bm.shape[-2:]
    r = h // g
    l = SSD_CHUNK
    c = s // l
    x = (xh.astype(jnp.float32) * dt[..., None]).reshape(bsz, c, l, g, r, p)
    bm = bm.astype(jnp.float32).reshape(bsz, c, l, g, n)
    cm = cm.astype(jnp.float32).reshape(bsz, c, l, g, n)
    a_cs = jnp.cumsum((dt * a).reshape(bsz, c, l, g, r), axis=2)
    causal = jnp.tril(jnp.ones((l, l), dtype=bool))[:, :, None, None]
    seg = a_cs[:, :, :, None] - a_cs[:, :, None, :]
    decay = jnp.exp(jnp.where(causal, seg, -jnp.inf))
    cb = jnp.einsum("bclgn,bcsgn->bclsg", cm, bm)
    y_diag = jnp.einsum("bclsgr,bcsgrp->bclgrp", cb[..., None] * decay, x)
    decay_to_end = jnp.exp(a_cs[:, :, -1:] - a_cs)
    chunk_states = jnp.einsum("bclgn,bclgrp->bcgrpn", bm, x * decay_to_end[..., None])
    chunk_decay = jnp.exp(a_cs[:, :, -1])

    def step(state, inp):
        dec, st = inp
        return state * dec[..., None, None] + st, state

    init = jnp.zeros((bsz, g, r, p, n), jnp.float32)
    _, prev = lax.scan(step, init, (jnp.moveaxis(chunk_decay, 1, 0), jnp.moveaxis(chunk_states, 1, 0)))
    prev = jnp.moveaxis(prev, 0, 1)
    y_off = jnp.einsum("bclgn,bcgrpn->bclgrp", cm, prev) * jnp.exp(a_cs)[..., None]
    return (y_diag + y_off).reshape(bsz, s, h, p)


def ssd_branch(z, xbc, dt_raw, conv_w, conv_b, dt_bias, a_log, d_skip, norm_g):
    bsz, s, _ = z.shape
    xbc = jax.nn.silu(causal_dwconv(xbc, conv_w, conv_b))
    xs, bm, cm = jnp.split(xbc, (SSD_D_INNER, SSD_D_INNER + SSD_GROUPS * SSD_STATE), axis=-1)
    xh = xs.reshape(bsz, s, SSD_HEADS, SSD_HEAD_DIM)
    bm = bm.reshape(bsz, s, SSD_GROUPS, SSD_STATE)
    cm = cm.reshape(bsz, s, SSD_GROUPS, SSD_STATE)
    dt = jax.nn.softplus(dt_raw.astype(jnp.float32) + dt_bias.astype(jnp.float32))
    a = -jnp.exp(a_log.astype(jnp.float32))
    y = ssd_chunked(xh, dt, a, bm, cm) + d_skip.astype(jnp.float32)[:, None] * xh.astype(jnp.float32)
    yg = (y.reshape(bsz, s, SSD_D_INNER) * jax.nn.silu(z.astype(jnp.float32))).reshape(bsz, s, SSD_GROUPS, -1)
    yg = yg * lax.rsqrt(jnp.mean(yg * yg, axis=-1, keepdims=True) + EPS)
    y = yg.reshape(bsz, s, SSD_D_INNER) * norm_g.astype(jnp.float32)
    return y.astype(z.dtype)


def gmlp_branch(uv, v_norm_g, w_s, b_s):
    bsz, s, _ = uv.shape
    u, v = jnp.split(jax.nn.gelu(uv, approximate=False), 2, axis=-1)
    v = rmsnorm(v, v_norm_g)
    vb = v.reshape(bsz, s // GMLP_CHUNK, GMLP_CHUNK, GMLP_GROUPS, GMLP_GROUP_DIM)
    mask = jnp.tril(jnp.ones((GMLP_CHUNK, GMLP_CHUNK), dtype=bool))
    ws = jnp.where(mask[None], w_s, 0).astype(v.dtype)
    mixed = jnp.einsum("gts,bcsgd->bctgd", ws, vb) + b_s.T.astype(v.dtype)[None, None, :, :, None]
    return u * mixed.reshape(bsz, s, GMLP_WIDTH)


def memory_attention(q, mem_n, w_mem_kv):
    bsz, s, _ = q.shape
    k, v = jnp.split(mem_n @ w_mem_kv, 2, axis=-1)
    q = q.reshape(bsz, s, MEM_HEADS, MEM_HEAD_DIM)
    k = k.reshape(bsz, -1, MEM_HEADS, MEM_HEAD_DIM)
    v = v.reshape(bsz, -1, MEM_HEADS, MEM_HEAD_DIM)
    scores = jnp.einsum("bshd,bmhd->bhsm", q, k).astype(jnp.float32) * (1.0 / math.sqrt(MEM_HEAD_DIM))
    probs = jax.nn.softmax(scores, axis=-1).astype(v.dtype)
    return jnp.einsum("bhsm,bmhd->bshd", probs, v).reshape(bsz, s, MEM_WIDTH)


def setup_inputs(seed: int = 0) -> dict:
    key = jax.random.key(seed)
    ks = iter(jax.random.split(key, 40))

    def nrm(shape, scale):
        return jax.random.normal(next(ks), shape, jnp.float32) * scale

    def gain(shape):
        return 1.0 + nrm(shape, 0.02)

    d = DEPTH
    dt0 = jnp.exp(jax.random.uniform(next(ks), (d, SSD_HEADS), jnp.float32, math.log(1e-3), math.log(1e-1)))
    dt_bias = dt0 + jnp.log(-jnp.expm1(-dt0))
    a_log = jnp.log(jax.random.uniform(next(ks), (d, SSD_HEADS), jnp.float32, 1.0, 16.0))
    return {
        "x": nrm((BATCH, SEQ, D_MODEL), 1.0),
        "mem": nrm((BATCH, MEM_LEN, D_MODEL), 1.0),
        "ffn1_norm": gain((d, D_MODEL)),
        "ffn1_w_gate": nrm((d, D_MODEL, D_FF), D_MODEL ** -0.5),
        "ffn1_w_up": nrm((d, D_MODEL, D_FF), D_MODEL ** -0.5),
        "ffn1_w_down": nrm((d, D_FF, D_MODEL), D_FF ** -0.5),
        "mix_norm": gain((d, D_MODEL)),
        "mem_norm": gain((d, D_MODEL)),
        "w_in": nrm((d, D_MODEL, IN_WIDTH), D_MODEL ** -0.5),
        "ssd_conv_w": nrm((d, SSD_CONV, SSD_CONV_DIM), SSD_CONV ** -0.5),
        "ssd_conv_b": nrm((d, SSD_CONV_DIM), 0.01),
        "ssd_dt_bias": dt_bias,
        "ssd_a_log": a_log,
        "ssd_d": gain((d, SSD_HEADS)),
        "ssd_norm": gain((d, SSD_D_INNER)),
        "gmlp_v_norm": gain((d, GMLP_WIDTH)),
        "gmlp_w_s": nrm((d, GMLP_GROUPS, GMLP_CHUNK, GMLP_CHUNK), GMLP_CHUNK ** -0.5),
        "gmlp_b_s": gain((d, GMLP_GROUPS, GMLP_CHUNK)),
        "w_mem_kv": nrm((d, D_MODEL, 2 * MEM_WIDTH), D_MODEL ** -0.5),
        "w_branch_ssd": nrm((d, SSD_D_INNER, D_MODEL), SSD_D_INNER ** -0.5),
        "w_branch_gmlp": nrm((d, GMLP_WIDTH, D_MODEL), GMLP_WIDTH ** -0.5),
        "w_branch_mem": nrm((d, MEM_WIDTH, D_MODEL), MEM_WIDTH ** -0.5),
        "w_out": nrm((d, D_MODEL, D_MODEL), D_MODEL ** -0.5),
        "ffn2_norm": gain((d, D_MODEL)),
        "ffn2_w_gate": nrm((d, D_MODEL, D_FF), D_MODEL ** -0.5),
        "ffn2_w_up": nrm((d, D_MODEL, D_FF), D_MODEL ** -0.5),
        "ffn2_w_down": nrm((d, D_FF, D_MODEL), D_FF ** -0.5),
        "final_norm": gain((D_MODEL,)),
    }


def reference(x, mem, ffn1_norm, ffn1_w_gate, ffn1_w_up, ffn1_w_down, mix_norm, mem_norm, w_in,
              ssd_conv_w, ssd_conv_b, ssd_dt_bias, ssd_a_log, ssd_d, ssd_norm,
              gmlp_v_norm, gmlp_w_s, gmlp_b_s, w_mem_kv,
              w_branch_ssd, w_branch_gmlp, w_branch_mem, w_out,
              ffn2_norm, ffn2_w_gate, ffn2_w_up, ffn2_w_down, final_norm):
    h = x
    for layer in range(DEPTH):
        h = h + 0.5 * swiglu(rmsnorm(h, ffn1_norm[layer]), ffn1_w_gate[layer], ffn1_w_up[layer], ffn1_w_down[layer])
        n = rmsnorm(h, mix_norm[layer])
        z, xbc, dt_raw, uv, q_mem, gate_logits = jnp.split(n @ w_in[layer], _IN_SPLITS, axis=-1)
        y_ssd = ssd_branch(z, xbc, dt_raw, ssd_conv_w[layer], ssd_conv_b[layer], ssd_dt_bias[layer],
                           ssd_a_log[layer], ssd_d[layer], ssd_norm[layer])
        y_gmlp = gmlp_branch(uv, gmlp_v_norm[layer], gmlp_w_s[layer], gmlp_b_s[layer])
        y_mem = memory_attention(q_mem, rmsnorm(mem, mem_norm[layer]), w_mem_kv[layer])
        g_ssd, g_gmlp, g_mem = jnp.split(jax.nn.sigmoid(gate_logits), N_BRANCH, axis=-1)
        merged = (g_ssd * (y_ssd @ w_branch_ssd[layer])
                  + g_gmlp * (y_gmlp @ w_branch_gmlp[layer])
                  + g_mem * (y_mem @ w_branch_mem[layer]))
        h = h + merged @ w_out[layer]
        h = h + 0.5 * swiglu(rmsnorm(h, ffn2_norm[layer]), ffn2_w_gate[layer], ffn2_w_up[layer], ffn2_w_down[layer])
    return rmsnorm(h, final_norm)
```

```python
import functools
import math

import jax
import jax.numpy as jnp
from jax import lax
from jax.experimental import pallas as pl
from jax.experimental.pallas import tpu as pltpu

F32 = jnp.float32
BF16 = jnp.bfloat16

D_MODEL = 1024
MEM_LEN = 256
SSD_D_INNER = 2048
SSD_HEAD_DIM = 64
SSD_HEADS = 32
SSD_GROUPS = 4
SSD_STATE = 128
SSD_CONV = 4
CHUNK = 128
SSD_BC = SSD_GROUPS * SSD_STATE
SSD_CONV_DIM = SSD_D_INNER + 2 * SSD_BC
GROUP_W = SSD_D_INNER // SSD_GROUPS
GMLP_WIDTH = 1024
GMLP_GROUPS = 8
MEM_HEADS = 4
MEM_HEAD_DIM = 64
MEM_WIDTH = 256
D_FF = 2816
EPS = 1e-6

LANES = 128
SUBLANES = 8
HALO = SUBLANES

FFN_TILE = 512
MIX_TILE = 256
VMEM_LIMIT = 60 * 1024 * 1024


def _rms(x, g):
    return x * lax.rsqrt(jnp.mean(x * x, axis=-1, keepdims=True) + EPS) * g


def _dot(a, b):
    return jnp.dot(a, b, preferred_element_type=F32)


def _const_spec(shape):
    nd = len(shape)
    return pl.BlockSpec(shape, lambda *_: (0,) * nd, pipeline_mode=pl.Buffered(1))


def _ffn_kernel(x_ref, g_ref, wgu_ref, wd_ref, fg_ref, o_ref, *, final_norm):
    x = x_ref[...]
    xn = _rms(x, g_ref[...]).astype(BF16)
    gu = _dot(xn, wgu_ref[...])
    gate = gu[:, :D_FF]
    up = gu[:, D_FF:]
    act = (gate * jax.nn.sigmoid(gate) * up).astype(BF16)
    h = x + 0.5 * _dot(act, wd_ref[...])
    if final_norm:
        h = _rms(h, fg_ref[...])
    o_ref[...] = h


def _ffn(x2d, norm_g, w_gu, w_down, final_g, *, final_norm):
    t, d = x2d.shape
    return pl.pallas_call(
        functools.partial(_ffn_kernel, final_norm=final_norm),
        grid=(t // FFN_TILE,),
        in_specs=[
            pl.BlockSpec((FFN_TILE, d), lambda i: (i, 0)),
            _const_spec((1, d)),
            _const_spec((d, 2 * D_FF)),
            _const_spec((D_FF, d)),
            _const_spec((1, d)),
        ],
        out_specs=pl.BlockSpec((FFN_TILE, d), lambda i: (i, 0)),
        out_shape=jax.ShapeDtypeStruct((t, d), F32),
        compiler_params=pltpu.CompilerParams(
            dimension_semantics=("arbitrary",), vmem_limit_bytes=VMEM_LIMIT),
        name="ffn_final" if final_norm else "ffn",
    )(x2d, norm_g, w_gu, w_down, final_g)


def _cumsum_rows(x):
    n = x.shape[0]
    row = lax.broadcasted_iota(jnp.int32, x.shape, 0)
    sh = 1
    while sh < n:
        x = x + jnp.where(row >= sh, pltpu.roll(x, sh, axis=0), 0.0)
        sh *= 2
    return x


def _lane_bcast(x, col):
    return jnp.broadcast_to(x[:, col:col + 1], (x.shape[0], LANES))


def _mix_kernel(h_ref, mem_ref, mixg_ref, memg_ref,
                wz_ref, wxbc_ref, wdt_ref, wuv_ref, wq_ref, wg_ref,
                convw_ref, convb_ref, dtb_ref, alog_ref, dskip_ref, ssdg_ref,
                gvg_ref, ws_ref, bsf_ref, wkv_ref,
                wbs_ref, wbg_ref, wbm_ref, wout_ref,
                o_ref,
                n_s, z_s, xbc_s, dt_s, uv_s, q_s,
                yssd_s, ygm_s, ymem_s,
                st_s, kbd_s, vbd_s, wsm_s,
                xdt_s, eacs_s, yc_s):
    i = pl.program_id(1)
    tile = h_ref.shape[0]

    @pl.when(i == 0)
    def _init():
        st_s[...] = jnp.zeros_like(st_s)
        xbc_s[0:HALO, :] = jnp.zeros((HALO, SSD_CONV_DIM), F32)
        r = lax.broadcasted_iota(jnp.int32, (CHUNK, CHUNK), 0)
        c = lax.broadcasted_iota(jnp.int32, (CHUNK, CHUNK), 1)
        for g in range(GMLP_GROUPS):
            wsm_s[g] = jnp.where(r >= c, ws_ref[g], 0.0).astype(BF16)
        mem_n = _rms(mem_ref[...], memg_ref[...]).astype(BF16)
        kv = _dot(mem_n, wkv_ref[...])
        k_t = kv[:, :MEM_WIDTH].T
        v = kv[:, MEM_WIDTH:]
        kbd_s[...] = jnp.zeros_like(kbd_s)
        vcol = lax.broadcasted_iota(jnp.int32, (MEM_LEN, MEM_WIDTH), 1)
        for hd in range(MEM_HEADS):
            lo = hd * MEM_HEAD_DIM
            kbd_s[lo:lo + MEM_HEAD_DIM, hd * MEM_LEN:(hd + 1) * MEM_LEN] = (
                k_t[lo:lo + MEM_HEAD_DIM, :].astype(BF16))
            in_head = (vcol >= lo) & (vcol < lo + MEM_HEAD_DIM)
            vbd_s[hd * MEM_LEN:(hd + 1) * MEM_LEN, :] = jnp.where(in_head, v, 0.0).astype(BF16)

    h = h_ref[...]
    n_s[...] = _rms(h, mixg_ref[...]).astype(BF16)
    n = n_s[...]
    z_s[...] = _dot(n, wz_ref[...])
    xbc_s[HALO:HALO + tile, :] = _dot(n, wxbc_ref[...])
    dt_s[...] = _dot(n, wdt_ref[...])
    uv_s[...] = _dot(n, wuv_ref[...])
    q_s[...] = _dot(n, wq_ref[...])

    row = lax.broadcasted_iota(jnp.int32, (CHUNK, CHUNK), 0)
    col = lax.broadcasted_iota(jnp.int32, (CHUNK, CHUNK), 1)
    causal = row >= col
    low_half = col < SSD_HEAD_DIM
    a_neg = -jnp.exp(alog_ref[...])

    for c in range(tile // CHUNK):
        r0 = c * CHUNK
        cw = convw_ref[...]
        conv = convb_ref[...] + cw[SSD_CONV - 1:SSD_CONV, :] * xbc_s[HALO + r0:HALO + r0 + CHUNK, :]
        for k in range(SSD_CONV - 1):
            back = SSD_CONV - 1 - k
            conv = conv + cw[k:k + 1, :] * xbc_s[HALO + r0 - back:HALO + r0 - back + CHUNK, :]
        xbc = conv * jax.nn.sigmoid(conv)
        xs = xbc[:, :SSD_D_INNER]
        bm = xbc[:, SSD_D_INNER:SSD_D_INNER + SSD_BC]
        cm = xbc[:, SSD_D_INNER + SSD_BC:]

        dtr = dt_s[r0:r0 + CHUNK, :] + dtb_ref[...]
        dt = jnp.maximum(dtr, 0.0) + jnp.log1p(jnp.exp(-jnp.abs(dtr)))
        acs = _cumsum_rows(dt * a_neg)
        acs_t = acs.T

        cm_b = cm.astype(BF16)
        bm_t = [bm[:, g * SSD_STATE:(g + 1) * SSD_STATE].T.astype(BF16) for g in range(SSD_GROUPS)]
        cb = [_dot(cm_b[:, g * SSD_STATE:(g + 1) * SSD_STATE], bm_t[g]) for g in range(SSD_GROUPS)]

        for j in range(SSD_HEADS // 2):
            h0, h1 = 2 * j, 2 * j + 1
            g = h0 // (SSD_HEADS // SSD_GROUPS)
            sl = slice(j * LANES, (j + 1) * LANES)
            ca0, ca1 = _lane_bcast(acs, h0), _lane_bcast(acs, h1)
            eacs_s[:, sl] = jnp.where(low_half, ca0, ca1)
            e_dt = jnp.where(low_half, _lane_bcast(dt, h0), _lane_bcast(dt, h1))
            xdt = xs[:, sl] * e_dt
            xdt_s[:, sl] = xdt
            m0 = cb[g] * jnp.exp(jnp.where(causal, ca0 - acs_t[h0:h0 + 1, :], -jnp.inf))
            m1 = cb[g] * jnp.exp(jnp.where(causal, ca1 - acs_t[h1:h1 + 1, :], -jnp.inf))
            lhs = jnp.concatenate([m0, m1], axis=1).astype(BF16)
            rhs = jnp.concatenate([jnp.where(low_half, xdt, 0.0),
                                   jnp.where(low_half, 0.0, xdt)], axis=0).astype(BF16)
            yc_s[:, sl] = _dot(lhs, rhs)

        eacs = eacs_s[...]
        a_end = eacs[CHUNK - 1:CHUNK, :]
        y = yc_s[...] + dskip_ref[...] * xs
        xdte = (xdt_s[...] * jnp.exp(a_end - eacs)).astype(BF16)
        st_old = st_s[...]
        y_off = jnp.concatenate(
            [_dot(cm_b[:, g * SSD_STATE:(g + 1) * SSD_STATE],
                  st_old[:, g * GROUP_W:(g + 1) * GROUP_W].astype(BF16)) for g in range(SSD_GROUPS)],
            axis=1)
        y = y + y_off * jnp.exp(eacs)
        st_new = jnp.concatenate(
            [_dot(bm_t[g], xdte[:, g * GROUP_W:(g + 1) * GROUP_W]) for g in range(SSD_GROUPS)], axis=1)
        st_s[...] = st_old * jnp.exp(a_end) + st_new

        z = z_s[r0:r0 + CHUNK, :]
        yg = y * (z * jax.nn.sigmoid(z))
        parts = []
        for g in range(SSD_GROUPS):
            p = yg[:, g * GROUP_W:(g + 1) * GROUP_W]
            parts.append(p * lax.rsqrt(jnp.mean(p * p, axis=-1, keepdims=True) + EPS))
        yssd_s[r0:r0 + CHUNK, :] = (jnp.concatenate(parts, axis=1) * ssdg_ref[...]).astype(BF16)

        uv = uv_s[r0:r0 + CHUNK, :]
        ge = 0.5 * uv * (1.0 + lax.erf(uv * (1.0 / math.sqrt(2.0))))
        u = ge[:, :GMLP_WIDTH]
        vn = _rms(ge[:, GMLP_WIDTH:], gvg_ref[...]).astype(BF16)
        mixed = jnp.concatenate(
            [_dot(wsm_s[g], vn[:, g * LANES:(g + 1) * LANES]) for g in range(GMLP_GROUPS)], axis=1)
        ygm_s[r0:r0 + CHUNK, :] = (u * (mixed + bsf_ref[...])).astype(BF16)

        q = q_s[r0:r0 + CHUNK, :].astype(BF16)
        sc = _dot(q, kbd_s[...]) * (1.0 / math.sqrt(MEM_HEAD_DIM))
        probs = []
        for hd in range(MEM_HEADS):
            s_h = sc[:, hd * MEM_LEN:(hd + 1) * MEM_LEN]
            e = jnp.exp(s_h - jnp.max(s_h, axis=-1, keepdims=True))
            probs.append(e / jnp.sum(e, axis=-1, keepdims=True))
        ymem_s[r0:r0 + CHUNK, :] = _dot(jnp.concatenate(probs, axis=1).astype(BF16), vbd_s[...]).astype(BF16)

    xbc_s[0:HALO, :] = xbc_s[tile:tile + HALO, :]

    merged = (jax.nn.sigmoid(_dot(n, wg_ref[:, 0:D_MODEL])) * _dot(yssd_s[...], wbs_ref[...])
              + jax.nn.sigmoid(_dot(n, wg_ref[:, D_MODEL:2 * D_MODEL])) * _dot(ygm_s[...], wbg_ref[...])
              + jax.nn.sigmoid(_dot(n, wg_ref[:, 2 * D_MODEL:])) * _dot(ymem_s[...], wbm_ref[...]))
    o_ref[...] = h + _dot(merged.astype(BF16), wout_ref[...])


def _mix(h1, mem, mixg, memg, wz, wxbc, wdt, wuv, wq, wg, convw, convb, dtb, alog, dskip, ssdg,
         gvg, ws, bsf, wkv, wbs, wbg, wbm, wout):
    b, s, d = h1.shape
    t = MIX_TILE
    consts = (mixg, memg, wz, wxbc, wdt, wuv, wq, wg, convw, convb, dtb, alog, dskip, ssdg,
              gvg, ws, bsf, wkv, wbs, wbg, wbm, wout)
    scratch = [
        pltpu.VMEM((t, d), BF16),
        pltpu.VMEM((t, SSD_D_INNER), F32),
        pltpu.VMEM((t + HALO, SSD_CONV_DIM), F32),
        pltpu.VMEM((t, LANES), F32),
        pltpu.VMEM((t, 2 * GMLP_WIDTH), F32),
        pltpu.VMEM((t, MEM_WIDTH), F32),
        pltpu.VMEM((t, SSD_D_INNER), BF16),
        pltpu.VMEM((t, GMLP_WIDTH), BF16),
        pltpu.VMEM((t, MEM_WIDTH), BF16),
        pltpu.VMEM((SSD_STATE, SSD_D_INNER), F32),
        pltpu.VMEM((MEM_WIDTH, MEM_HEADS * MEM_LEN), BF16),
        pltpu.VMEM((MEM_HEADS * MEM_LEN, MEM_WIDTH), BF16),
        pltpu.VMEM((GMLP_GROUPS, CHUNK, CHUNK), BF16),
        pltpu.VMEM((CHUNK, SSD_D_INNER), F32),
        pltpu.VMEM((CHUNK, SSD_D_INNER), F32),
        pltpu.VMEM((CHUNK, SSD_D_INNER), F32),
    ]
    return pl.pallas_call(
        _mix_kernel,
        grid=(b, s // t),
        in_specs=[pl.BlockSpec((None, t, d), lambda bi, i: (bi, i, 0)),
                  pl.BlockSpec((None, MEM_LEN, d), lambda bi, i: (bi, 0, 0))]
                 + [_const_spec(c.shape) for c in consts],
        out_specs=pl.BlockSpec((None, t, d), lambda bi, i: (bi, i, 0)),
        out_shape=jax.ShapeDtypeStruct((b, s, d), F32),
        scratch_shapes=scratch,
        compiler_params=pltpu.CompilerParams(
            dimension_semantics=("arbitrary", "arbitrary"), vmem_limit_bytes=VMEM_LIMIT),
        name="mix",
    )(h1, mem, *consts)


def _pad_lanes(a):
    return jnp.pad(a, ((0, 0), (0, LANES - a.shape[1])))


def kernel(x, mem, ffn1_norm, ffn1_w_gate, ffn1_w_up, ffn1_w_down, mix_norm, mem_norm, w_in, ssd_conv_w, ssd_conv_b, ssd_dt_bias, ssd_a_log, ssd_d, ssd_norm, gmlp_v_norm, gmlp_w_s, gmlp_b_s, w_mem_kv, w_branch_ssd, w_branch_gmlp, w_branch_mem, w_out, ffn2_norm, ffn2_w_gate, ffn2_w_up, ffn2_w_down, final_norm):
    b, s, d = x.shape
    depth = ffn1_norm.shape[0]
    final_g = final_norm.reshape(1, d)
    h = x
    for l in range(depth):
        last = l == depth - 1
        o_z, o_xbc = SSD_D_INNER, SSD_D_INNER + SSD_CONV_DIM
        o_dt = o_xbc + SSD_HEADS
        o_uv = o_dt + 2 * GMLP_WIDTH
        o_q = o_uv + MEM_WIDTH
        wi = w_in[l]
        wz = wi[:, :o_z].astype(BF16)
        wxbc = wi[:, o_z:o_xbc].astype(BF16)
        wdt = _pad_lanes(wi[:, o_xbc:o_dt]).astype(BF16)
        wuv = wi[:, o_dt:o_uv].astype(BF16)
        wq = wi[:, o_uv:o_q].astype(BF16)
        wg = wi[:, o_q:].astype(BF16)

        w_gu1 = jnp.concatenate([ffn1_w_gate[l], ffn1_w_up[l]], axis=1).astype(BF16)
        h = _ffn(h.reshape(b * s, d), ffn1_norm[l].reshape(1, d), w_gu1, ffn1_w_down[l].astype(BF16),
                 final_g, final_norm=False).reshape(b, s, d)

        h = _mix(
            h, mem, mix_norm[l].reshape(1, d), mem_norm[l].reshape(1, d),
            wz, wxbc, wdt, wuv, wq, wg,
            ssd_conv_w[l], ssd_conv_b[l].reshape(1, -1),
            _pad_lanes(ssd_dt_bias[l].reshape(1, -1)), _pad_lanes(ssd_a_log[l].reshape(1, -1)),
            jnp.repeat(ssd_d[l], SSD_HEAD_DIM).reshape(1, -1), ssd_norm[l].reshape(1, -1),
            gmlp_v_norm[l].reshape(1, -1), gmlp_w_s[l],
            jnp.repeat(gmlp_b_s[l].T, GMLP_WIDTH // GMLP_GROUPS, axis=1),
            w_mem_kv[l].astype(BF16),
            w_branch_ssd[l].astype(BF16), w_branch_gmlp[l].astype(BF16),
            w_branch_mem[l].astype(BF16), w_out[l].astype(BF16))

        w_gu2 = jnp.concatenate([ffn2_w_gate[l], ffn2_w_up[l]], axis=1).astype(BF16)
        h = _ffn(h.reshape(b * s, d), ffn2_norm[l].reshape(1, d), w_gu2, ffn2_w_down[l].astype(BF16),
                 final_g, final_norm=last).reshape(b, s, d)
    if depth == 0:
        h = x * lax.rsqrt(jnp.mean(x * x, axis=-1, keepdims=True) + EPS) * final_norm
    return h
```

```python
import functools
import math

import jax
import jax.numpy as jnp
from jax import lax
from jax.experimental import pallas as pl
from jax.experimental.pallas import tpu as pltpu

F32 = jnp.float32
BF16 = jnp.bfloat16

D_MODEL = 1024
MEM_LEN = 256
SSD_D_INNER = 2048
SSD_HEAD_DIM = 64
SSD_HEADS = 32
SSD_GROUPS = 4
SSD_STATE = 128
SSD_CONV = 4
CHUNK = 128
SSD_BC = SSD_GROUPS * SSD_STATE
SSD_CONV_DIM = SSD_D_INNER + 2 * SSD_BC
GROUP_W = SSD_D_INNER // SSD_GROUPS
GMLP_WIDTH = 1024
GMLP_GROUPS = 8
MEM_HEADS = 4
MEM_HEAD_DIM = 64
MEM_WIDTH = 256
D_FF = 2816
EPS = 1e-6
LOG2E = math.log2(math.e)

LANES = 128
SUBLANES = 8
HALO = SUBLANES
CONV_SLABS = SSD_CONV_DIM // LANES

FFN_TILE = 512
MIX_TILE = 256
VMEM_LIMIT = 60 * 1024 * 1024
PIECE = 512

IN_Z = 0
IN_XBC = IN_Z + SSD_D_INNER
IN_UV = IN_XBC + SSD_CONV_DIM
IN_Q = IN_UV + 2 * GMLP_WIDTH
IN_GATES = IN_Q + MEM_WIDTH
IN_DT = IN_GATES + 3 * D_MODEL
IN_COLS = IN_DT + LANES
WIDE_COLS = 1024


def _rms(x, g):
    return x * lax.rsqrt(jnp.mean(x * x, axis=-1, keepdims=True) + EPS) * g


def _dot(a, b):
    return jnp.dot(a, b, preferred_element_type=F32)


def _const_spec(shape):
    nd = len(shape)
    return pl.BlockSpec(shape, lambda *_: (0,) * nd, pipeline_mode=pl.Buffered(1))


def _ffn_kernel(x_ref, g_ref, wg_ref, wu_ref, wd_ref, fg_ref, o_ref, *, final_norm):
    x = x_ref[...]
    xn = _rms(x, g_ref[...]).astype(BF16)
    gate = _dot(xn, wg_ref[...])
    up = _dot(xn, wu_ref[...])
    act = (gate * jax.nn.sigmoid(gate) * up).astype(BF16)
    h = x + 0.5 * _dot(act, wd_ref[:, :x.shape[1]])
    if final_norm:
        h = _rms(h, fg_ref[...])
    o_ref[...] = h


def _ffn(x2d, norm_g, w_gate, w_up, w_down, final_g, *, final_norm):
    t, d = x2d.shape
    return pl.pallas_call(
        functools.partial(_ffn_kernel, final_norm=final_norm),
        grid=(t // FFN_TILE,),
        in_specs=[
            pl.BlockSpec((FFN_TILE, d), lambda i: (i, 0)),
            _const_spec((1, d)),
            _const_spec((d, D_FF)),
            _const_spec((d, D_FF)),
            _const_spec(w_down.shape),
            _const_spec((1, d)),
        ],
        out_specs=pl.BlockSpec((FFN_TILE, d), lambda i: (i, 0)),
        out_shape=jax.ShapeDtypeStruct((t, d), F32),
        compiler_params=pltpu.CompilerParams(
            dimension_semantics=("arbitrary",), vmem_limit_bytes=VMEM_LIMIT),
        name="ffn_final" if final_norm else "ffn",
    )(x2d, norm_g, w_gate, w_up, w_down, final_g)


def _cumsum_rows(x):
    n = x.shape[0]
    row = lax.broadcasted_iota(jnp.int32, x.shape, 0)
    sh = 1
    while sh < n:
        x = x + jnp.where(row >= sh, pltpu.roll(x, sh, axis=0), 0.0)
        sh *= 2
    return x


def _lane_bcast(x, col):
    return jnp.broadcast_to(x[:, col:col + 1], (x.shape[0], LANES))


def _mix_kernel(h_ref, hnext_ref, mem_ref, mixg_ref, memg_ref,
                win_ref,
                convw_ref, convb_ref, dtb_ref, alog_ref, dskip_ref, ssdg_ref,
                gvg_ref, ws_ref, bsf_ref, wkv_ref,
                wbs_ref, wbg_ref, wbm_ref, wout_ref,
                o_ref,
                n_s, xbc_s, dt_s, z_a, z_b, uv_a, uv_b, q_a, q_b, g_a, g_b,
                xc_s, yssd_s, ygm_s, ymem_s, mrg_s,
                st_s, kbd_s, vbd_s, wsm_s,
                xdt_s, eacs_s, yc_s):
    i = pl.program_id(1)
    n_chunks = h_ref.shape[0] // CHUNK
    z_sets, uv_sets, q_sets, g_sets = (z_a, z_b), (uv_a, uv_b), (q_a, q_b), (g_a, g_b)

    def proj_norm(h_rows):
        n_s[...] = _rms(h_rows, mixg_ref[...]).astype(BF16)

    def proj(col0, width):
        return _dot(n_s[...], win_ref[:, col0:col0 + width])

    def proj_pieces(par):
        def xbc_piece(j):
            r = proj(IN_XBC + j, PIECE)
            for k in range(PIECE // LANES):
                xbc_s[j // LANES + k, HALO:HALO + CHUNK, :] = r[:, k * LANES:(k + 1) * LANES]

        def into(dst_ref, col0, j, width):
            dst_ref[:, j:j + width] = proj(col0 + j, width)

        part = functools.partial
        pieces = [part(xbc_piece, j) for j in range(0, SSD_CONV_DIM, PIECE)]
        pieces.append(part(into, dt_s, IN_DT, 0, LANES))
        pieces += [part(into, z_sets[par], IN_Z, j, PIECE) for j in range(0, SSD_D_INNER, PIECE)]
        pieces += [part(into, uv_sets[par], IN_UV, j, PIECE) for j in range(0, 2 * GMLP_WIDTH, PIECE)]
        pieces.append(part(into, q_sets[par], IN_Q, 0, MEM_WIDTH))
        pieces += [part(into, g_sets[par], IN_GATES, j, PIECE) for j in range(0, 3 * D_MODEL, PIECE)]
        return pieces

    def conv_slab(s):
        ls = slice(s * LANES, (s + 1) * LANES)
        conv = convb_ref[:, ls] + convw_ref[SSD_CONV - 1:SSD_CONV, ls] * xbc_s[s, HALO:HALO + CHUNK, :]
        for k in range(SSD_CONV - 1):
            back = SSD_CONV - 1 - k
            conv = conv + convw_ref[k:k + 1, ls] * xbc_s[s, HALO - back:HALO - back + CHUNK, :]
        xc_s[:, ls] = conv * jax.nn.sigmoid(conv)
        xbc_s[s, 0:HALO, :] = xbc_s[s, CHUNK:CHUNK + HALO, :]

    @pl.when(i == 0)
    def _init():
        st_s[...] = jnp.zeros_like(st_s)
        for s in range(CONV_SLABS):
            xbc_s[s, 0:HALO, :] = jnp.zeros((HALO, LANES), F32)
        r = lax.broadcasted_iota(jnp.int32, (CHUNK, CHUNK), 0)
        c = lax.broadcasted_iota(jnp.int32, (CHUNK, CHUNK), 1)
        for g in range(GMLP_GROUPS):
            wsm_s[g] = jnp.where(r >= c, ws_ref[g], 0.0).astype(BF16)
        mem_n = _rms(mem_ref[...], memg_ref[...]).astype(BF16)
        kv = _dot(mem_n, wkv_ref[...])
        k_t = kv[:, :MEM_WIDTH].T
        v = kv[:, MEM_WIDTH:]
        kbd_s[...] = jnp.zeros_like(kbd_s)
        vcol = lax.broadcasted_iota(jnp.int32, (MEM_LEN, MEM_WIDTH), 1)
        for hd in range(MEM_HEADS):
            lo = hd * MEM_HEAD_DIM
            kbd_s[lo:lo + MEM_HEAD_DIM, hd * MEM_LEN:(hd + 1) * MEM_LEN] = (
                k_t[lo:lo + MEM_HEAD_DIM, :].astype(BF16))
            in_head = (vcol >= lo) & (vcol < lo + MEM_HEAD_DIM)
            vbd_s[hd * MEM_LEN:(hd + 1) * MEM_LEN, :] = jnp.where(in_head, v, 0.0).astype(BF16)
        proj_norm(h_ref[0:CHUNK, :])
        for piece in proj_pieces(0):
            piece()
        for s in range(CONV_SLABS):
            conv_slab(s)

    row = lax.broadcasted_iota(jnp.int32, (CHUNK, CHUNK), 0)
    col = lax.broadcasted_iota(jnp.int32, (CHUNK, CHUNK), 1)
    causal = row >= col
    low_half = col < SSD_HEAD_DIM
    a_neg = -jnp.exp(alog_ref[...])
    pairs_per_group = SSD_HEADS // 2 // SSD_GROUPS

    for c in range(n_chunks):
        r0 = c * CHUNK
        par = c % 2
        z_cur, uv_cur, q_cur, g_cur = z_sets[par], uv_sets[par], q_sets[par], g_sets[par]
        h_next = h_ref[r0 + CHUNK:r0 + 2 * CHUNK, :] if c + 1 < n_chunks else hnext_ref[...]

        dtr = dt_s[...] + dtb_ref[...]
        proj_norm(h_next)
        pieces = iter(proj_pieces(1 - par))

        dt = jnp.maximum(dtr, 0.0) + jnp.log1p(jnp.exp(-jnp.abs(dtr)))
        acs = _cumsum_rows(dt * a_neg) * LOG2E
        acs_t = acs.T

        cm_b = xc_s[:, SSD_D_INNER + SSD_BC:].astype(BF16)
        bm_t = [xc_s[:, SSD_D_INNER + g * SSD_STATE:SSD_D_INNER + (g + 1) * SSD_STATE].T.astype(BF16)
                for g in range(SSD_GROUPS)]

        for j in range(SSD_HEADS // 2):
            h0, h1 = 2 * j, 2 * j + 1
            g = j // pairs_per_group
            if j % pairs_per_group == 0:
                cb = _dot(cm_b[:, g * SSD_STATE:(g + 1) * SSD_STATE], bm_t[g])
            sl = slice(j * LANES, (j + 1) * LANES)
            ca0, ca1 = _lane_bcast(acs, h0), _lane_bcast(acs, h1)
            eacs_s[:, sl] = jnp.where(low_half, ca0, ca1)
            e_dt = jnp.where(low_half, _lane_bcast(dt, h0), _lane_bcast(dt, h1))
            xdt = xc_s[:, sl] * e_dt
            xdt_s[:, sl] = xdt
            m0 = cb * jnp.exp2(jnp.where(causal, ca0 - acs_t[h0:h0 + 1, :], -jnp.inf))
            m1 = cb * jnp.exp2(jnp.where(causal, ca1 - acs_t[h1:h1 + 1, :], -jnp.inf))
            lhs = jnp.concatenate([m0, m1], axis=1).astype(BF16)
            rhs = jnp.concatenate([jnp.where(low_half, xdt, 0.0),
                                   jnp.where(low_half, 0.0, xdt)], axis=0).astype(BF16)
            yc_s[:, sl] = _dot(lhs, rhs)
            next(pieces)()

        for g in range(SSD_GROUPS):
            gs = slice(g * GROUP_W, (g + 1) * GROUP_W)
            eacs = eacs_s[:, gs]
            a_end = eacs[CHUNK - 1:CHUNK, :]
            st_old = st_s[:, gs]
            y_off = _dot(cm_b[:, g * SSD_STATE:(g + 1) * SSD_STATE], st_old.astype(BF16))
            y = yc_s[:, gs] + dskip_ref[:, gs] * xc_s[:, gs] + y_off * jnp.exp2(eacs)
            xdte = (xdt_s[:, gs] * jnp.exp2(a_end - eacs)).astype(BF16)
            st_s[:, gs] = st_old * jnp.exp2(a_end) + _dot(bm_t[g], xdte)
            z = z_cur[:, gs]
            yg = y * (z * jax.nn.sigmoid(z))
            yg = yg * lax.rsqrt(jnp.mean(yg * yg, axis=-1, keepdims=True) + EPS)
            yssd_s[:, gs] = (yg * ssdg_ref[:, gs]).astype(BF16)
            next(pieces)()

        uv = uv_cur[...]
        ge = 0.5 * uv * (1.0 + lax.erf(uv * (1.0 / math.sqrt(2.0))))
        u = ge[:, :GMLP_WIDTH]
        vn = _rms(ge[:, GMLP_WIDTH:], gvg_ref[...]).astype(BF16)
        mixed = jnp.concatenate(
            [_dot(wsm_s[g], vn[:, g * LANES:(g + 1) * LANES]) for g in range(GMLP_GROUPS)], axis=1)
        ygm_s[...] = (u * (mixed + bsf_ref[...])).astype(BF16)
        next(pieces)()

        q = q_cur[...].astype(BF16)
        sc = _dot(q, kbd_s[...]) * (1.0 / math.sqrt(MEM_HEAD_DIM))
        probs = []
        for hd in range(MEM_HEADS):
            s_h = sc[:, hd * MEM_LEN:(hd + 1) * MEM_LEN]
            e = jnp.exp(s_h - jnp.max(s_h, axis=-1, keepdims=True))
            probs.append(e / jnp.sum(e, axis=-1, keepdims=True))
        ymem_s[...] = _dot(jnp.concatenate(probs, axis=1).astype(BF16), vbd_s[...]).astype(BF16)
        next(pieces)()

        slabs = iter(range(CONV_SLABS))
        n_blocks = D_MODEL // PIECE
        slabs_per_block = CONV_SLABS // (2 * n_blocks)
        for blk in range(n_blocks):
            bs = slice(blk * PIECE, (blk + 1) * PIECE)
            merged = (jax.nn.sigmoid(g_cur[:, bs]) * _dot(yssd_s[...], wbs_ref[:, bs])
                      + jax.nn.sigmoid(g_cur[:, D_MODEL + blk * PIECE:D_MODEL + (blk + 1) * PIECE])
                      * _dot(ygm_s[...], wbg_ref[:, bs])
                      + jax.nn.sigmoid(g_cur[:, 2 * D_MODEL + blk * PIECE:2 * D_MODEL + (blk + 1) * PIECE])
                      * _dot(ymem_s[...], wbm_ref[:, bs]))
            mrg_s[:, bs] = merged.astype(BF16)
            for _ in range(slabs_per_block):
                conv_slab(next(slabs))
        for blk in range(n_blocks):
            bs = slice(blk * PIECE, (blk + 1) * PIECE)
            o_ref[r0:r0 + CHUNK, bs] = h_ref[r0:r0 + CHUNK, bs] + _dot(mrg_s[...], wout_ref[:, bs])
            for _ in range(slabs_per_block):
                conv_slab(next(slabs))


def _mix(h1, mem, mixg, memg, win, convw, convb, dtb, alog, dskip, ssdg,
         gvg, ws, bsf, wkv, wbs, wbg, wbm, wout):
    b, s, d = h1.shape
    t = MIX_TILE
    chunks_per_tile = t // CHUNK
    assert chunks_per_tile % 2 == 0
    last_chunk = s // CHUNK - 1
    consts = (mixg, memg, win, convw, convb, dtb, alog, dskip, ssdg,
              gvg, ws, bsf, wkv, wbs, wbg, wbm, wout)
    two = lambda shape, dtype: [pltpu.VMEM(shape, dtype), pltpu.VMEM(shape, dtype)]
    scratch = (
        [pltpu.VMEM((CHUNK, d), BF16),
         pltpu.VMEM((CONV_SLABS, CHUNK + HALO, LANES), F32),
         pltpu.VMEM((CHUNK, LANES), F32)]
        + two((CHUNK, SSD_D_INNER), F32)
        + two((CHUNK, 2 * GMLP_WIDTH), F32)
        + two((CHUNK, MEM_WIDTH), F32)
        + two((CHUNK, 3 * D_MODEL), F32)
        + [pltpu.VMEM((CHUNK, SSD_CONV_DIM), F32),
           pltpu.VMEM((CHUNK, SSD_D_INNER), BF16),
           pltpu.VMEM((CHUNK, GMLP_WIDTH), BF16),
           pltpu.VMEM((CHUNK, MEM_WIDTH), BF16),
           pltpu.VMEM((CHUNK, D_MODEL), BF16),
           pltpu.VMEM((SSD_STATE, SSD_D_INNER), F32),
           pltpu.VMEM((MEM_WIDTH, MEM_HEADS * MEM_LEN), BF16),
           pltpu.VMEM((MEM_HEADS * MEM_LEN, MEM_WIDTH), BF16),
           pltpu.VMEM((GMLP_GROUPS, CHUNK, CHUNK), BF16),
           pltpu.VMEM((CHUNK, SSD_D_INNER), F32),
           pltpu.VMEM((CHUNK, SSD_D_INNER), F32),
           pltpu.VMEM((CHUNK, SSD_D_INNER), F32)])
    return pl.pallas_call(
        _mix_kernel,
        grid=(b, s // t),
        in_specs=[pl.BlockSpec((None, t, d), lambda bi, i: (bi, i, 0)),
                  pl.BlockSpec((None, CHUNK, d),
                               lambda bi, i: (bi, jnp.minimum((i + 1) * chunks_per_tile, last_chunk), 0)),
                  pl.BlockSpec((None, MEM_LEN, d), lambda bi, i: (bi, 0, 0))]
                 + [_const_spec(c.shape) for c in consts],
        out_specs=pl.BlockSpec((None, t, d), lambda bi, i: (bi, i, 0)),
        out_shape=jax.ShapeDtypeStruct((b, s, d), F32),
        scratch_shapes=scratch,
        compiler_params=pltpu.CompilerParams(
            dimension_semantics=("arbitrary", "arbitrary"), vmem_limit_bytes=VMEM_LIMIT),
        name="mix",
    )(h1, h1, mem, *consts)


def _pad_lanes(a):
    return jnp.pad(a, ((0, 0), (0, LANES - a.shape[1])))


def _mxu_weight(w):
    w = w.astype(BF16)
    if w.shape[1] % WIDE_COLS == 0:
        w = jnp.pad(w, ((0, 0), (0, LANES)))
    return w


def kernel(x, mem, ffn1_norm, ffn1_w_gate, ffn1_w_up, ffn1_w_down, mix_norm, mem_norm, w_in, ssd_conv_w, ssd_conv_b, ssd_dt_bias, ssd_a_log, ssd_d, ssd_norm, gmlp_v_norm, gmlp_w_s, gmlp_b_s, w_mem_kv, w_branch_ssd, w_branch_gmlp, w_branch_mem, w_out, ffn2_norm, ffn2_w_gate, ffn2_w_up, ffn2_w_down, final_norm):
    b, s, d = x.shape
    depth = ffn1_norm.shape[0]
    assert depth >= 1
    final_g = final_norm.reshape(1, d)
    o_xbc = SSD_D_INNER + SSD_CONV_DIM
    o_dt = o_xbc + SSD_HEADS
    h = x
    for l in range(depth):
        wi = w_in[l]
        h = _ffn(h.reshape(b * s, d), ffn1_norm[l].reshape(1, d),
                 ffn1_w_gate[l].astype(BF16), ffn1_w_up[l].astype(BF16), _mxu_weight(ffn1_w_down[l]),
                 final_g, final_norm=False).reshape(b, s, d)

        win = jnp.concatenate(
            [wi[:, :o_xbc], wi[:, o_dt:], _pad_lanes(wi[:, o_xbc:o_dt])], axis=1).astype(BF16)
        assert win.shape[1] == IN_COLS and IN_COLS % WIDE_COLS != 0
        h = _mix(
            h, mem, mix_norm[l].reshape(1, d), mem_norm[l].reshape(1, d), win,
            ssd_conv_w[l], ssd_conv_b[l].reshape(1, -1),
            _pad_lanes(ssd_dt_bias[l].reshape(1, -1)), _pad_lanes(ssd_a_log[l].reshape(1, -1)),
            jnp.repeat(ssd_d[l], SSD_HEAD_DIM).reshape(1, -1), ssd_norm[l].reshape(1, -1),
            gmlp_v_norm[l].reshape(1, -1), gmlp_w_s[l],
            jnp.repeat(gmlp_b_s[l].T, GMLP_WIDTH // GMLP_GROUPS, axis=1),
            w_mem_kv[l].astype(BF16),
            _mxu_weight(w_branch_ssd[l]), _mxu_weight(w_branch_gmlp[l]),
            _mxu_weight(w_branch_mem[l]), _mxu_weight(w_out[l]))

        h = _ffn(h.reshape(b * s, d), ffn2_norm[l].reshape(1, d),
                 ffn2_w_gate[l].astype(BF16), ffn2_w_up[l].astype(BF16), _mxu_weight(ffn2_w_down[l]),
                 final_g, final_norm=(l == depth - 1)).reshape(b, s, d)
    return h
```

```python
import functools
import math

import jax
import jax.numpy as jnp
from jax import lax
from jax.experimental import pallas as pl
from jax.experimental.pallas import tpu as pltpu

F32 = jnp.float32
BF16 = jnp.bfloat16

D_MODEL = 1024
MEM_LEN = 256
SSD_D_INNER = 2048
SSD_HEAD_DIM = 64
SSD_HEADS = 32
SSD_GROUPS = 4
SSD_STATE = 128
SSD_CONV = 4
CHUNK = 128
SSD_BC = SSD_GROUPS * SSD_STATE
SSD_CONV_DIM = SSD_D_INNER + 2 * SSD_BC
GROUP_W = SSD_D_INNER // SSD_GROUPS
GMLP_WIDTH = 1024
GMLP_GROUPS = 8
MEM_HEADS = 4
MEM_HEAD_DIM = 64
MEM_WIDTH = 256
D_FF = 2816
EPS = 1e-6
LOG2E = math.log2(math.e)

LANES = 128
SUBLANES = 8
HALO = SUBLANES
CONV_SLABS = SSD_CONV_DIM // LANES

FFN_TILE = 512
MIX_TILE = 256
V7X_VMEM_BYTES = 64 * 1024 * 1024
VMEM_LIMIT = V7X_VMEM_BYTES - 4 * 1024 * 1024
PIECE = 256
IN_PIECE = 256

IN_A_Z = 0
IN_A_XBC = IN_A_Z + SSD_D_INNER
IN_A_DT = IN_A_XBC + SSD_CONV_DIM
IN_A_COLS = IN_A_DT + LANES
IN_B_UV = 0
IN_B_Q = IN_B_UV + 2 * GMLP_WIDTH
IN_B_GATES = IN_B_Q + MEM_WIDTH
IN_B_COLS = IN_B_GATES + 3 * D_MODEL
WIDE_COLS = 1024


def _rms(x, g):
    return x * lax.rsqrt(jnp.mean(x * x, axis=-1, keepdims=True) + EPS) * g


def _dot(a, b):
    return jnp.dot(a, b, preferred_element_type=F32)


def _const_spec(shape):
    nd = len(shape)
    return pl.BlockSpec(shape, lambda *_: (0,) * nd, pipeline_mode=pl.Buffered(1))


def _ffn_kernel(x_ref, g_ref, wg_ref, wu_ref, wd_ref, fg_ref, o_ref, *, final_norm):
    x = x_ref[...]
    xn = _rms(x, g_ref[...]).astype(BF16)
    gate = _dot(xn, wg_ref[...])
    up = _dot(xn, wu_ref[...])
    act = (gate * jax.nn.sigmoid(gate) * up).astype(BF16)
    h = x + 0.5 * _dot(act, wd_ref[...])
    if final_norm:
        h = _rms(h, fg_ref[...])
    o_ref[...] = h


def _ffn(x2d, norm_g, w_gate, w_up, w_down, final_g, *, final_norm):
    t, d = x2d.shape
    return pl.pallas_call(
        functools.partial(_ffn_kernel, final_norm=final_norm),
        grid=(t // FFN_TILE,),
        in_specs=[
            pl.BlockSpec((FFN_TILE, d), lambda i: (i, 0)),
            _const_spec((1, d)),
            _const_spec((d, D_FF)),
            _const_spec((d, D_FF)),
            _const_spec(w_down.shape),
            _const_spec((1, d)),
        ],
        out_specs=pl.BlockSpec((FFN_TILE, d), lambda i: (i, 0)),
        out_shape=jax.ShapeDtypeStruct((t, d), F32),
        compiler_params=pltpu.CompilerParams(
            dimension_semantics=("arbitrary",), vmem_limit_bytes=VMEM_LIMIT),
        name="ffn_final" if final_norm else "ffn",
    )(x2d, norm_g, w_gate, w_up, w_down, final_g)


def _cumsum_rows(x):
    n = x.shape[0]
    row = lax.broadcasted_iota(jnp.int32, x.shape, 0)
    sh = 1
    while sh < n:
        x = x + jnp.where(row >= sh, pltpu.roll(x, sh, axis=0), 0.0)
        sh *= 2
    return x


def _lane_bcast(x, col):
    return jnp.broadcast_to(x[:, col:col + 1], (x.shape[0], LANES))


def _mix_kernel(h_ref, hnext_ref, mem_ref, mixg_ref, memg_ref,
                wina_ref, winb_ref,
                convw_ref, convb_ref, dtb_ref, alog_ref, dskip_ref, ssdg_ref,
                gvg_ref, ws_ref, bsf_ref, wkv_ref,
                wbs_ref, wbg_ref, wbm_ref, wout_ref,
                o_ref,
                n_s, xbc_s, dt_s, z_a, z_b, uv_a, uv_b, q_a, q_b, g_a, g_b,
                xc_s, yssd_s, ygm_s, ymem_s, mrg_s,
                st_s, kbd_s, vbd_s, wsm_s,
                xdt_s, eacs_s, yc_s):
    i = pl.program_id(1)
    n_chunks = h_ref.shape[0] // CHUNK
    z_sets, uv_sets, q_sets, g_sets = (z_a, z_b), (uv_a, uv_b), (q_a, q_b), (g_a, g_b)

    def proj_norm(h_rows):
        n_s[...] = _rms(h_rows, mixg_ref[...]).astype(BF16)

    def proj(w_ref, col0, width):
        return _dot(n_s[...], w_ref[:, col0:col0 + width])

    def proj_pieces(par):
        def xbc_piece(j):
            r = proj(wina_ref, IN_A_XBC + j, IN_PIECE)
            for k in range(IN_PIECE // LANES):
                xbc_s[j // LANES + k, HALO:HALO + CHUNK, :] = r[:, k * LANES:(k + 1) * LANES]

        def into(dst_ref, w_ref, col0, j, width):
            dst_ref[:, j:j + width] = proj(w_ref, col0 + j, width)

        part = functools.partial
        pieces = [part(xbc_piece, j) for j in range(0, SSD_CONV_DIM, IN_PIECE)]
        pieces.append(part(into, dt_s, wina_ref, IN_A_DT, 0, LANES))
        pieces += [part(into, z_sets[par], wina_ref, IN_A_Z, j, IN_PIECE) for j in range(0, SSD_D_INNER, IN_PIECE)]
        pieces += [part(into, uv_sets[par], winb_ref, IN_B_UV, j, IN_PIECE)
                   for j in range(0, 2 * GMLP_WIDTH, IN_PIECE)]
        pieces.append(part(into, q_sets[par], winb_ref, IN_B_Q, 0, MEM_WIDTH))
        pieces += [part(into, g_sets[par], winb_ref, IN_B_GATES, j, IN_PIECE) for j in range(0, 3 * D_MODEL, IN_PIECE)]
        return pieces

    def conv_slab(s):
        ls = slice(s * LANES, (s + 1) * LANES)
        conv = convb_ref[:, ls] + convw_ref[SSD_CONV - 1:SSD_CONV, ls] * xbc_s[s, HALO:HALO + CHUNK, :]
        for k in range(SSD_CONV - 1):
            back = SSD_CONV - 1 - k
            conv = conv + convw_ref[k:k + 1, ls] * xbc_s[s, HALO - back:HALO - back + CHUNK, :]
        xc_s[:, ls] = conv * jax.nn.sigmoid(conv)
        xbc_s[s, 0:HALO, :] = xbc_s[s, CHUNK:CHUNK + HALO, :]

    @pl.when(i == 0)
    def _init():
        st_s[...] = jnp.zeros_like(st_s)
        for s in range(CONV_SLABS):
            xbc_s[s, 0:HALO, :] = jnp.zeros((HALO, LANES), F32)
        r = lax.broadcasted_iota(jnp.int32, (CHUNK, CHUNK), 0)
        c = lax.broadcasted_iota(jnp.int32, (CHUNK, CHUNK), 1)
        for g in range(GMLP_GROUPS):
            wsm_s[g] = jnp.where(r >= c, ws_ref[g], 0.0).astype(BF16)
        mem_n = _rms(mem_ref[...], memg_ref[...]).astype(BF16)
        kv = _dot(mem_n, wkv_ref[...])
        k_t = kv[:, :MEM_WIDTH].T
        v = kv[:, MEM_WIDTH:]
        kbd_s[...] = jnp.zeros_like(kbd_s)
        vcol = lax.broadcasted_iota(jnp.int32, (MEM_LEN, MEM_WIDTH), 1)
        for hd in range(MEM_HEADS):
            lo = hd * MEM_HEAD_DIM
            kbd_s[lo:lo + MEM_HEAD_DIM, hd * MEM_LEN:(hd + 1) * MEM_LEN] = (
                k_t[lo:lo + MEM_HEAD_DIM, :].astype(BF16))
            in_head = (vcol >= lo) & (vcol < lo + MEM_HEAD_DIM)
            vbd_s[hd * MEM_LEN:(hd + 1) * MEM_LEN, :] = jnp.where(in_head, v, 0.0).astype(BF16)
        proj_norm(h_ref[0:CHUNK, :])
        for piece in proj_pieces(0):
            piece()
        for s in range(CONV_SLABS):
            conv_slab(s)

    row = lax.broadcasted_iota(jnp.int32, (CHUNK, CHUNK), 0)
    col = lax.broadcasted_iota(jnp.int32, (CHUNK, CHUNK), 1)
    causal = row >= col
    low_half = col < SSD_HEAD_DIM
    a_neg = -jnp.exp(alog_ref[...])
    pairs_per_group = SSD_HEADS // 2 // SSD_GROUPS

    for c in range(n_chunks):
        r0 = c * CHUNK
        par = c % 2
        z_cur, uv_cur, q_cur, g_cur = z_sets[par], uv_sets[par], q_sets[par], g_sets[par]
        h_next = h_ref[r0 + CHUNK:r0 + 2 * CHUNK, :] if c + 1 < n_chunks else hnext_ref[...]

        dtr = dt_s[...] + dtb_ref[...]
        proj_norm(h_next)
        pieces = iter(proj_pieces(1 - par))

        def emit(k=2):
            for _ in range(k):
                piece = next(pieces, None)
                if piece is not None:
                    piece()

        dt = jnp.maximum(dtr, 0.0) + jnp.log1p(jnp.exp(-jnp.abs(dtr)))
        acs = _cumsum_rows(dt * a_neg) * LOG2E
        acs_t = acs.T

        cm_b = xc_s[:, SSD_D_INNER + SSD_BC:].astype(BF16)
        bm_t = [xc_s[:, SSD_D_INNER + g * SSD_STATE:SSD_D_INNER + (g + 1) * SSD_STATE].T.astype(BF16)
                for g in range(SSD_GROUPS)]

        for j in range(SSD_HEADS // 2):
            h0, h1 = 2 * j, 2 * j + 1
            g = j // pairs_per_group
            if j % pairs_per_group == 0:
                cb = _dot(cm_b[:, g * SSD_STATE:(g + 1) * SSD_STATE], bm_t[g])
            sl = slice(j * LANES, (j + 1) * LANES)
            emit(1)
            ca0, ca1 = _lane_bcast(acs, h0), _lane_bcast(acs, h1)
            eacs_s[:, sl] = jnp.where(low_half, ca0, ca1)
            e_dt = jnp.where(low_half, _lane_bcast(dt, h0), _lane_bcast(dt, h1))
            xdt = xc_s[:, sl] * e_dt
            xdt_s[:, sl] = xdt
            emit(1)
            m0 = cb * jnp.exp2(jnp.where(causal, ca0 - acs_t[h0:h0 + 1, :], -jnp.inf))
            m1 = cb * jnp.exp2(jnp.where(causal, ca1 - acs_t[h1:h1 + 1, :], -jnp.inf))
            lhs = jnp.concatenate([m0, m1], axis=1).astype(BF16)
            rhs = jnp.concatenate([jnp.where(low_half, xdt, 0.0),
                                   jnp.where(low_half, 0.0, xdt)], axis=0).astype(BF16)
            yc_s[:, sl] = _dot(lhs, rhs)

        for g in range(SSD_GROUPS):
            gs = slice(g * GROUP_W, (g + 1) * GROUP_W)
            eacs = eacs_s[:, gs]
            a_end = eacs[CHUNK - 1:CHUNK, :]
            st_old = st_s[:, gs]
            y_off = _dot(cm_b[:, g * SSD_STATE:(g + 1) * SSD_STATE], st_old.astype(BF16))
            y = yc_s[:, gs] + dskip_ref[:, gs] * xc_s[:, gs] + y_off * jnp.exp2(eacs)
            xdte = (xdt_s[:, gs] * jnp.exp2(a_end - eacs)).astype(BF16)
            st_s[:, gs] = st_old * jnp.exp2(a_end) + _dot(bm_t[g], xdte)
            z = z_cur[:, gs]
            yg = y * (z * jax.nn.sigmoid(z))
            yg = yg * lax.rsqrt(jnp.mean(yg * yg, axis=-1, keepdims=True) + EPS)
            yssd_s[:, gs] = (yg * ssdg_ref[:, gs]).astype(BF16)
            emit(2)

        uv = uv_cur[...]
        ge = 0.5 * uv * (1.0 + lax.erf(uv * (1.0 / math.sqrt(2.0))))
        u = ge[:, :GMLP_WIDTH]
        vn = _rms(ge[:, GMLP_WIDTH:], gvg_ref[...]).astype(BF16)
        emit(1)
        mixed = jnp.concatenate(
            [_dot(wsm_s[g], vn[:, g * LANES:(g + 1) * LANES]) for g in range(GMLP_GROUPS)], axis=1)
        ygm_s[...] = (u * (mixed + bsf_ref[...])).astype(BF16)
        emit(1)

        q = q_cur[...].astype(BF16)
        sc = _dot(q, kbd_s[...]) * (1.0 / math.sqrt(MEM_HEAD_DIM))
        probs = []
        for hd in range(MEM_HEADS):
            s_h = sc[:, hd * MEM_LEN:(hd + 1) * MEM_LEN]
            e = jnp.exp(s_h - jnp.max(s_h, axis=-1, keepdims=True))
            probs.append(e / jnp.sum(e, axis=-1, keepdims=True))
        ymem_s[...] = _dot(jnp.concatenate(probs, axis=1).astype(BF16), vbd_s[...]).astype(BF16)
        emit()

        for piece in pieces:
            piece()
        slabs = iter(range(CONV_SLABS))
        n_blocks = D_MODEL // PIECE
        slabs_per_block = CONV_SLABS // (2 * n_blocks)
        for blk in range(n_blocks):
            bs = slice(blk * PIECE, (blk + 1) * PIECE)
            merged = (jax.nn.sigmoid(g_cur[:, bs]) * _dot(yssd_s[...], wbs_ref[:, bs])
                      + jax.nn.sigmoid(g_cur[:, D_MODEL + blk * PIECE:D_MODEL + (blk + 1) * PIECE])
                      * _dot(ygm_s[...], wbg_ref[:, bs])
                      + jax.nn.sigmoid(g_cur[:, 2 * D_MODEL + blk * PIECE:2 * D_MODEL + (blk + 1) * PIECE])
                      * _dot(ymem_s[...], wbm_ref[:, bs]))
            mrg_s[:, bs] = merged.astype(BF16)
            for _ in range(slabs_per_block):
                conv_slab(next(slabs))
        for blk in range(n_blocks):
            bs = slice(blk * PIECE, (blk + 1) * PIECE)
            o_ref[r0:r0 + CHUNK, bs] = h_ref[r0:r0 + CHUNK, bs] + _dot(mrg_s[...], wout_ref[:, bs])
            for _ in range(slabs_per_block):
                conv_slab(next(slabs))


def _mix(h1, mem, mixg, memg, wina, winb, convw, convb, dtb, alog, dskip, ssdg,
         gvg, ws, bsf, wkv, wbs, wbg, wbm, wout):
    b, s, d = h1.shape
    t = MIX_TILE
    chunks_per_tile = t // CHUNK
    assert chunks_per_tile % 2 == 0
    last_chunk = s // CHUNK - 1
    consts = (mixg, memg, wina, winb, convw, convb, dtb, alog, dskip, ssdg,
              gvg, ws, bsf, wkv, wbs, wbg, wbm, wout)
    two = lambda shape, dtype: [pltpu.VMEM(shape, dtype), pltpu.VMEM(shape, dtype)]
    scratch = (
        [pltpu.VMEM((CHUNK, d), BF16),
         pltpu.VMEM((CONV_SLABS, CHUNK + HALO, LANES), F32),
         pltpu.VMEM((CHUNK, LANES), F32)]
        + two((CHUNK, SSD_D_INNER), F32)
        + two((CHUNK, 2 * GMLP_WIDTH), F32)
        + two((CHUNK, MEM_WIDTH), F32)
        + two((CHUNK, 3 * D_MODEL), F32)
        + [pltpu.VMEM((CHUNK, SSD_CONV_DIM), F32),
           pltpu.VMEM((CHUNK, SSD_D_INNER), BF16),
           pltpu.VMEM((CHUNK, GMLP_WIDTH), BF16),
           pltpu.VMEM((CHUNK, MEM_WIDTH), BF16),
           pltpu.VMEM((CHUNK, D_MODEL), BF16),
           pltpu.VMEM((SSD_STATE, SSD_D_INNER), F32),
           pltpu.VMEM((MEM_WIDTH, MEM_HEADS * MEM_LEN), BF16),
           pltpu.VMEM((MEM_HEADS * MEM_LEN, MEM_WIDTH), BF16),
           pltpu.VMEM((GMLP_GROUPS, CHUNK, CHUNK), BF16),
           pltpu.VMEM((CHUNK, SSD_D_INNER), F32),
           pltpu.VMEM((CHUNK, SSD_D_INNER), F32),
           pltpu.VMEM((CHUNK, SSD_D_INNER), F32)])
    return pl.pallas_call(
        _mix_kernel,
        grid=(b, s // t),
        in_specs=[pl.BlockSpec((None, t, d), lambda bi, i: (bi, i, 0)),
                  pl.BlockSpec((None, CHUNK, d),
                               lambda bi, i: (bi, jnp.minimum((i + 1) * chunks_per_tile, last_chunk), 0)),
                  pl.BlockSpec((None, MEM_LEN, d), lambda bi, i: (bi, 0, 0))]
                 + [_const_spec(c.shape) for c in consts],
        out_specs=pl.BlockSpec((None, t, d), lambda bi, i: (bi, i, 0)),
        out_shape=jax.ShapeDtypeStruct((b, s, d), F32),
        scratch_shapes=scratch,
        compiler_params=pltpu.CompilerParams(
            dimension_semantics=("arbitrary", "arbitrary"), vmem_limit_bytes=VMEM_LIMIT),
        name="mix",
    )(h1, h1, mem, *consts)


def _pad_lanes(a):
    return jnp.pad(a, ((0, 0), (0, LANES - a.shape[1])))


def _mxu_weight(w):
    w = w.astype(BF16)
    if w.shape[1] % WIDE_COLS == 0:
        w = jnp.pad(w, ((0, 0), (0, LANES)))
    return w


def kernel(x, mem, ffn1_norm, ffn1_w_gate, ffn1_w_up, ffn1_w_down, mix_norm, mem_norm, w_in, ssd_conv_w, ssd_conv_b, ssd_dt_bias, ssd_a_log, ssd_d, ssd_norm, gmlp_v_norm, gmlp_w_s, gmlp_b_s, w_mem_kv, w_branch_ssd, w_branch_gmlp, w_branch_mem, w_out, ffn2_norm, ffn2_w_gate, ffn2_w_up, ffn2_w_down, final_norm):
    b, s, d = x.shape
    depth = ffn1_norm.shape[0]
    assert depth >= 1
    final_g = final_norm.reshape(1, d)
    o_xbc = SSD_D_INNER + SSD_CONV_DIM
    o_dt = o_xbc + SSD_HEADS
    h = x
    for l in range(depth):
        wi = w_in[l]
        h = _ffn(h.reshape(b * s, d), ffn1_norm[l].reshape(1, d),
                 ffn1_w_gate[l].astype(BF16), ffn1_w_up[l].astype(BF16), ffn1_w_down[l].astype(BF16),
                 final_g, final_norm=False).reshape(b, s, d)

        wina = jnp.pad(wi[:, :o_dt], ((0, 0), (0, IN_A_COLS - o_dt))).astype(BF16)
        winb = wi[:, o_dt:].astype(BF16)
        assert winb.shape[1] == IN_B_COLS and IN_A_COLS % WIDE_COLS != 0 and IN_B_COLS % WIDE_COLS != 0
        h = _mix(
            h, mem, mix_norm[l].reshape(1, d), mem_norm[l].reshape(1, d), wina, winb,
            ssd_conv_w[l], ssd_conv_b[l].reshape(1, -1),
            _pad_lanes(ssd_dt_bias[l].reshape(1, -1)), _pad_lanes(ssd_a_log[l].reshape(1, -1)),
            jnp.repeat(ssd_d[l], SSD_HEAD_DIM).reshape(1, -1), ssd_norm[l].reshape(1, -1),
            gmlp_v_norm[l].reshape(1, -1), gmlp_w_s[l],
            jnp.repeat(gmlp_b_s[l].T, GMLP_WIDTH // GMLP_GROUPS, axis=1),
            w_mem_kv[l].astype(BF16),
            _mxu_weight(w_branch_ssd[l]), _mxu_weight(w_branch_gmlp[l]),
            _mxu_weight(w_branch_mem[l]), _mxu_weight(w_out[l]))

        h = _ffn(h.reshape(b * s, d), ffn2_norm[l].reshape(1, d),
                 ffn2_w_gate[l].astype(BF16), ffn2_w_up[l].astype(BF16), ffn2_w_down[l].astype(BF16),
                 final_g, final_norm=(l == depth - 1)).reshape(b, s, d)
    return h
```
